```python
import math
import jax, jax.numpy as jnp
from jax import lax
import numpy as np

D_MODEL = 2048
BATCH = 2
SEQ = 8192
DEPTH = 2

N_META = 16
CHUNK = 128
N_BRANCH = 4
BRANCH_W = D_MODEL // 4
EPS = 1e-6
LRU_BLOCKS = 8
LRU_BLOCK_DIM = BRANCH_W // LRU_BLOCKS
LRU_CONV = 4
LRU_PAD = (2, 1)
LRU_C = 8.0
HY_CONV = 3
HY_PAD = (1, 1)
HY_BANDS = 16
HY_EMB = 1 + 2 * HY_BANDS
HY_FFN = 64
HY_SIN_FREQ = 1.0
HY_DECAY_MIN = 3.07
HY_DECAY_MAX = 15.35
HY_FILTER_SCALE = 0.004
RET_HEADS = 4
RET_DK = BRANCH_W // RET_HEADS
RET_DV = BRANCH_W // RET_HEADS
ROPE_BASE = 10000.0
HG_HEADS = 4
HG_EXPAND = BRANCH_W // HG_HEADS
D_FF = -(-8 * D_MODEL // (3 * 256)) * 256
GATE_COLS = N_BRANCH * D_MODEL
MIX_COLS = 14 * BRANCH_W
N_IN_COLS = MIX_COLS + GATE_COLS
F32 = jnp.float32

kernel_name = "hybrid_rglru_hyena_retention_hgrn2_encoder"


def rms_norm(x, gain):
    xf = x.astype(F32)
    y = xf * lax.rsqrt(jnp.mean(xf * xf, axis=-1, keepdims=True) + EPS)
    return (y * gain.astype(F32)).astype(x.dtype)


def depthwise_conv(x, w, b, pad):
    y = lax.conv_general_dilated(x, w[:, None, :].astype(x.dtype), window_strides=(1,), padding=[pad],
                                 dimension_numbers=('NWC', 'WIO', 'NWC'), feature_group_count=x.shape[-1])
    return y + b.astype(x.dtype)


def split_projection(p):
    w = BRANCH_W
    sizes = (w, w, 3 * w, w, w, w, w, w, w, w, w, w, GATE_COLS)
    out, start = [], 0
    for s in sizes:
        out.append(p[..., start:start + s])
        start += s
    return out


def rglru_direction(xc, wa, ba, wx, bx, lam, reverse):
    bsz, t_len, width = xc.shape
    xb = xc.reshape(bsz, t_len, LRU_BLOCKS, LRU_BLOCK_DIM)
    gate_r = jax.nn.sigmoid(jnp.einsum('btki,kij->btkj', xb, wa).reshape(bsz, t_len, width).astype(F32) + ba.astype(F32))
    gate_i = jax.nn.sigmoid(jnp.einsum('btki,kij->btkj', xb, wx).reshape(bsz, t_len, width).astype(F32) + bx.astype(F32))
    log_a = -LRU_C * gate_r * jax.nn.softplus(-lam.astype(F32))
    a = jnp.exp(log_a)
    b = jnp.sqrt(-jnp.expm1(2.0 * log_a)) * gate_i * xc.astype(F32)

    def combine(left, right):
        a1, b1 = left
        a2, b2 = right
        return a1 * a2, a2 * b1 + b2

    _, h = lax.associative_scan(combine, (a, b), axis=1, reverse=reverse)
    return h


def rglru_branch(xa, ga, conv_w, conv_b, wa, ba, wx, bx, lam):
    xc = depthwise_conv(xa, conv_w, conv_b, LRU_PAD)
    h = (rglru_direction(xc, wa[0], ba[0], wx[0], bx[0], lam[0], False)
         + rglru_direction(xc, wa[1], ba[1], wx[1], bx[1], lam[1], True))
    return (h * jax.nn.gelu(ga.astype(F32))).astype(xa.dtype)


def hyena_filters(t_len, w1, b1, w2, b2, w3, decay):
    n = jnp.arange(t_len, dtype=F32)
    t = n / max(t_len - 1, 1)
    freqs = jnp.linspace(1e-4, HY_BANDS - 1, HY_BANDS, dtype=F32)
    ang = 2.0 * math.pi * n[:, None] * freqs[None, :] / t_len
    z = jnp.concatenate([t[:, None], jnp.cos(ang), -jnp.sin(ang)], axis=-1)
    h = jnp.sin(HY_SIN_FREQ * (z @ w1.astype(F32) + b1.astype(F32)))
    h = jnp.sin(HY_SIN_FREQ * (h @ w2.astype(F32) + b2.astype(F32)))
    h = h @ w3.astype(F32)
    return h * jnp.exp(-t[:, None] * jnp.abs(decay.astype(F32))[None, :])


def hyena_branch(u3, conv_w, conv_b, w1, b1, w2, b2, w3, decay, bias):
    u3 = depthwise_conv(u3, conv_w, conv_b, HY_PAD)
    x0, x1, v = jnp.split(u3, 3, axis=-1)
    t_len = u3.shape[1]
    filt = hyena_filters(t_len, w1, b1, w2, b2, w3, decay)
    h_fwd, h_bwd = filt[:, :BRANCH_W], filt[:, BRANCH_W:]
    k2 = jnp.concatenate([h_fwd, jnp.zeros((1, BRANCH_W), F32), h_bwd[:0:-1]], axis=0)
    u = (x1 * v).astype(F32)
    n_fft = 2 * t_len
    y = jnp.fft.irfft(jnp.fft.rfft(u, n=n_fft, axis=1) * jnp.fft.rfft(k2, n=n_fft, axis=0)[None],
                      n=n_fft, axis=1)[:, :t_len]
    y = y + u * bias.astype(F32)
    return (x0.astype(F32) * y).astype(u3.dtype)


def to_heads(x, n_heads):
    bsz, t_len, _ = x.shape
    return x.reshape(bsz, t_len, n_heads, -1).transpose(0, 2, 1, 3)


def from_heads(x):
    bsz, n_heads, t_len, d = x.shape
    return x.transpose(0, 2, 1, 3).reshape(bsz, t_len, n_heads * d)


def rotary(x):
    t_len, d = x.shape[2], x.shape[3]
    inv = ROPE_BASE ** (-jnp.arange(0, d, 2, dtype=F32) / d)
    ang = jnp.arange(t_len, dtype=F32)[:, None] * inv[None, :]
    cos, sin = jnp.cos(ang), jnp.sin(ang)
    xa, xb = x[..., : d // 2], x[..., d // 2:]
    return jnp.concatenate([xa * cos - xb * sin, xb * cos + xa * sin], axis=-1)


def gla_chunk_scan(q, k, v, log_f, inclusive):
    bsz, n_heads, t_len, dk = q.shape
    dv = v.shape[-1]
    n_chunks = t_len // CHUNK

    def chunks(a):
        return a.reshape(bsz, n_heads, n_chunks, CHUNK, a.shape[-1]).transpose(2, 0, 1, 3, 4)

    qc, kc, vc = chunks(q), chunks(k), chunks(v)
    bc = jnp.cumsum(chunks(log_f), axis=-2)
    idx = jnp.arange(CHUNK)
    mask = (idx[:, None] >= idx[None, :]) if inclusive else (idx[:, None] > idx[None, :])

    def step(state, inp):
        q_, k_, v_, b_ = inp
        diff = b_[..., :, None, :] - b_[..., None, :, :]
        dec = jnp.exp(jnp.where(mask[:, :, None], diff, -jnp.inf))
        scores = jnp.einsum('bhid,bhjd,bhijd->bhij', q_, k_, dec)
        out = (jnp.einsum('bhij,bhjv->bhiv', scores, v_)
               + jnp.einsum('bhid,bhdv->bhiv', q_ * jnp.exp(b_), state))
        b_last = b_[..., -1:, :]
        state = (jnp.exp(b_last[..., 0, :])[..., None] * state
                 + jnp.einsum('bhjd,bhjv->bhdv', k_ * jnp.exp(b_last - b_), v_))
        return state, out

    s0 = jnp.zeros((bsz, n_heads, dk, dv), F32)
    _, o = lax.scan(step, s0, (qc, kc, vc, bc))
    return o.transpose(1, 2, 0, 3, 4).reshape(bsz, n_heads, t_len, dv)


def bidirectional_gla(q, k_fwd, k_bwd, v, logf_fwd, logf_bwd, inclusive_bwd):
    pad = CHUNK - N_META

    def padf(a):
        return jnp.pad(a, ((0, 0), (0, 0), (pad, 0), (0, 0)))

    def flipf(a):
        return jnp.flip(padf(a), axis=2)

    fwd = gla_chunk_scan(padf(q), padf(k_fwd), padf(v), padf(logf_fwd), True)
    bwd = jnp.flip(gla_chunk_scan(flipf(q), flipf(k_bwd), flipf(v), flipf(logf_bwd), inclusive_bwd), axis=2)
    return (fwd + bwd)[:, :, pad:]


def retention_branch(q, k, v, g):
    qh = rotary(to_heads(q.astype(F32), RET_HEADS))
    kh = rotary(to_heads(k.astype(F32), RET_HEADS)) * (RET_DK ** -0.5)
    vh = to_heads(v.astype(F32), RET_HEADS)
    log_gamma = jnp.log1p(-(2.0 ** (-5.0 - jnp.arange(RET_HEADS, dtype=F32))))
    logf = jnp.broadcast_to(log_gamma[None, :, None, None], qh.shape)
    o = bidirectional_gla(qh, kh, kh, vh, logf, logf, inclusive_bwd=False)
    mu = jnp.mean(o, axis=-1, keepdims=True)
    o = (o - mu) * lax.rsqrt(jnp.mean((o - mu) ** 2, axis=-1, keepdims=True) + EPS)
    return (from_heads(o) * jax.nn.silu(g.astype(F32))).astype(q.dtype)


def hgrn_lower_bound(lb_logits, layer):
    c = jnp.cumsum(jax.nn.softmax(lb_logits.astype(F32), axis=0), axis=0)
    return c[layer] - c[0]


def hgrn2_branch(q, f_fwd, f_bwd, i, g, lb):
    qh = to_heads(jax.nn.silu(q.astype(F32)), HG_HEADS)
    ff = lb + (1.0 - lb) * jax.nn.sigmoid(f_fwd.astype(F32))
    fb = lb + (1.0 - lb) * jax.nn.sigmoid(f_bwd.astype(F32))
    o = bidirectional_gla(qh, to_heads(1.0 - ff, HG_HEADS), to_heads(1.0 - fb, HG_HEADS),
                          to_heads(i.astype(F32), HG_HEADS),
                          to_heads(jnp.log(ff), HG_HEADS), to_heads(jnp.log(fb), HG_HEADS), inclusive_bwd=True)
    o = o * lax.rsqrt(jnp.mean(o * o, axis=-1, keepdims=True) + EPS)
    return (from_heads(o) * jax.nn.silu(g.astype(F32))).astype(q.dtype)


def mixer_block(n, w_in, lru_conv_w, lru_conv_b, lru_wa, lru_ba, lru_wx, lru_bx, lru_lambda,
                hy_conv_w, hy_conv_b, hy_w1, hy_b1, hy_w2, hy_b2, hy_w3, hy_decay, hy_bias,
                lb, w_branch_out, w_out):
    bsz, t_len, _ = n.shape
    p = n @ w_in
    (a_x, a_g, b_u, c_q, c_k, c_v, c_g, d_q, d_ff, d_fb, d_i, d_g, gate_cols) = split_projection(p)
    za = rglru_branch(a_x, a_g, lru_conv_w, lru_conv_b, lru_wa, lru_ba, lru_wx, lru_bx, lru_lambda)
    zb = hyena_branch(b_u, hy_conv_w, hy_conv_b, hy_w1, hy_b1, hy_w2, hy_b2, hy_w3, hy_decay, hy_bias)
    zc = retention_branch(c_q, c_k, c_v, c_g)
    zd = hgrn2_branch(d_q, d_ff, d_fb, d_i, d_g, lb)
    z = jnp.stack([za, zb, zc, zd], axis=2)
    up = jnp.einsum('btnw,nwd->btnd', z, w_branch_out)
    gates = jax.nn.sigmoid(gate_cols.reshape(bsz, t_len, N_BRANCH, D_MODEL).astype(F32))
    merged = jnp.sum(gates * up.astype(F32), axis=2).astype(n.dtype)
    return merged @ w_out


def swiglu(n, w_gate, w_up, w_down):
    return (jax.nn.silu(n @ w_gate) * (n @ w_up)) @ w_down


def setup_inputs(seed: int = 0) -> dict:
    key = jax.random.key(seed)
    ks = jax.random.split(key, 28)
    L, W = DEPTH, BRANCH_W

    def normal(k, shape, scale):
        return jax.random.normal(k, shape, F32) * scale

    u_lam = jax.random.uniform(ks[11], (L, 2, W), F32, minval=0.9, maxval=0.999)
    a_lam = u_lam ** (1.0 / LRU_C)
    decay0 = jnp.tile(jnp.linspace(HY_DECAY_MIN, HY_DECAY_MAX, W, dtype=F32), 2)
    return {
        'x': normal(ks[0], (BATCH, SEQ, D_MODEL), 1.0),
        'meta': normal(ks[1], (N_META, D_MODEL), 1.0),
        'norm_mix': 1.0 + normal(ks[2], (L, D_MODEL), 0.02),
        'norm_ffn': 1.0 + normal(ks[3], (L, D_MODEL), 0.02),
        'w_in': normal(ks[4], (L, D_MODEL, N_IN_COLS), D_MODEL ** -0.5),
        'lru_conv_w': normal(ks[5], (L, LRU_CONV, W), LRU_CONV ** -0.5),
        'lru_conv_b': normal(ks[6], (L, W), 0.02),
        'lru_wa': normal(ks[7], (L, 2, LRU_BLOCKS, LRU_BLOCK_DIM, LRU_BLOCK_DIM), LRU_BLOCK_DIM ** -0.5),
        'lru_ba': normal(ks[8], (L, 2, W), 0.02),
        'lru_wx': normal(ks[9], (L, 2, LRU_BLOCKS, LRU_BLOCK_DIM, LRU_BLOCK_DIM), LRU_BLOCK_DIM ** -0.5),
        'lru_bx': normal(ks[10], (L, 2, W), 0.02),
        'lru_lambda': jnp.log(a_lam) - jnp.log1p(-a_lam),
        'hy_conv_w': normal(ks[12], (L, HY_CONV, 3 * W), HY_CONV ** -0.5),
        'hy_conv_b': normal(ks[13], (L, 3 * W), 0.02),
        'hy_w1': normal(ks[14], (L, HY_EMB, HY_FFN), HY_EMB ** -0.5),
        'hy_b1': normal(ks[15], (L, HY_FFN), 0.1),
        'hy_w2': normal(ks[16], (L, HY_FFN, HY_FFN), HY_FFN ** -0.5),
        'hy_b2': normal(ks[17], (L, HY_FFN), 0.1),
        'hy_w3': normal(ks[18], (L, HY_FFN, 2 * W), HY_FILTER_SCALE),
        'hy_decay': decay0[None, :] + normal(ks[19], (L, 2 * W), 0.1),
        'hy_bias': normal(ks[20], (L, W), 0.5),
        'hgrn_lb_logits': normal(ks[21], (L, W), 0.1),
        'w_branch_out': normal(ks[22], (L, N_BRANCH, W, D_MODEL), W ** -0.5),
        'w_out': normal(ks[23], (L, D_MODEL, D_MODEL), D_MODEL ** -0.5),
        'ffn_w_gate': normal(ks[24], (L, D_MODEL, D_FF), D_MODEL ** -0.5),
        'ffn_w_up': normal(ks[25], (L, D_MODEL, D_FF), D_MODEL ** -0.5),
        'ffn_w_down': normal(ks[26], (L, D_FF, D_MODEL), D_FF ** -0.5),
        'norm_final': 1.0 + normal(ks[27], (D_MODEL,), 0.02),
    }


def reference(x, meta, norm_mix, norm_ffn, w_in, lru_conv_w, lru_conv_b, lru_wa, lru_ba, lru_wx, lru_bx,
              lru_lambda, hy_conv_w, hy_conv_b, hy_w1, hy_b1, hy_w2, hy_b2, hy_w3, hy_decay, hy_bias,
              hgrn_lb_logits, w_branch_out, w_out, ffn_w_gate, ffn_w_up, ffn_w_down, norm_final):
    bsz = x.shape[0]
    h = jnp.concatenate([jnp.broadcast_to(meta[None].astype(x.dtype), (bsz, N_META, D_MODEL)), x], axis=1)
    for l in range(DEPTH):
        lb = hgrn_lower_bound(hgrn_lb_logits, l)
        h = h + mixer_block(rms_norm(h, norm_mix[l]), w_in[l], lru_conv_w[l], lru_conv_b[l], lru_wa[l],
                            lru_ba[l], lru_wx[l], lru_bx[l], lru_lambda[l], hy_conv_w[l], hy_conv_b[l],
                            hy_w1[l], hy_b1[l], hy_w2[l], hy_b2[l], hy_w3[l], hy_decay[l], hy_bias[l],
                            lb, w_branch_out[l], w_out[l])
        h = h + swiglu(rms_norm(h, norm_ffn[l]), ffn_w_gate[l], ffn_w_up[l], ffn_w_down[l])
    return rms_norm(h, norm_final)[:, N_META:]
```

```python
import functools
import math

import jax
import jax.numpy as jnp
from jax import lax
from jax.experimental import pallas as pl
from jax.experimental.pallas import tpu as pltpu

F32 = jnp.float32
BF16 = jnp.bfloat16

D_MODEL = 2048
N_META = 16
CHUNK = 128
N_BRANCH = 4
BRANCH_W = D_MODEL // 4
EPS = 1e-6
LRU_BLOCKS = 8
LRU_BLOCK_DIM = BRANCH_W // LRU_BLOCKS
LRU_PAD = (2, 1)
LRU_C = 8.0
HY_PAD = (1, 1)
HY_BANDS = 16
HY_SIN_FREQ = 1.0
RET_HEADS = 4
RET_DK = BRANCH_W // RET_HEADS
ROPE_BASE = 10000.0
HG_HEADS = 4
MIX_COLS = 14 * BRANCH_W
PAD = CHUNK - N_META

VMEM_LIMIT = 56 * 1024 * 1024
TM = 1280


def _cparams(sem):
    return pltpu.CompilerParams(dimension_semantics=sem, vmem_limit_bytes=VMEM_LIMIT)


def _rms_rows(x, gain):
    return x * lax.rsqrt(jnp.mean(x * x, axis=-1, keepdims=True) + EPS) * gain


def _norm_proj_kernel(h_ref, g_ref, w_ref, o_ref, n_ref):
    @pl.when(pl.program_id(1) == 0)
    def _():
        n_ref[...] = _rms_rows(h_ref[...], g_ref[...]).astype(BF16)

    o_ref[...] = jnp.dot(n_ref[...], w_ref[...], preferred_element_type=F32).astype(o_ref.dtype)


def norm_proj(h, gain, w, n_cols, tn, out_dtype):
    m, d = h.shape
    return pl.pallas_call(
        _norm_proj_kernel,
        grid=(m // TM, n_cols // tn),
        in_specs=[pl.BlockSpec((TM, d), lambda i, j: (i, 0)),
                  pl.BlockSpec((1, d), lambda i, j: (0, 0)),
                  pl.BlockSpec((d, tn), lambda i, j: (0, j))],
        out_specs=[pl.BlockSpec((TM, tn), lambda i, j: (i, j)),
                   pl.BlockSpec((TM, d), lambda i, j: (i, 0))],
        out_shape=[jax.ShapeDtypeStruct((m, n_cols), out_dtype),
                   jax.ShapeDtypeStruct((m, d), BF16)],
        compiler_params=_cparams(("parallel", "arbitrary")),
        name="norm_proj",
    )(h, gain.reshape(1, d), w)


def _merge_kernel(n_ref, z_ref, g0, g1, g2, g3, wbo_ref, o_ref):
    n = n_ref[...]
    acc = None
    for b, g_ref in enumerate((g0, g1, g2, g3)):
        gate = jax.nn.sigmoid(jnp.dot(n, g_ref[...], preferred_element_type=F32))
        up = jnp.dot(z_ref[:, b * BRANCH_W:(b + 1) * BRANCH_W], wbo_ref[b], preferred_element_type=F32)
        acc = gate * up if acc is None else acc + gate * up
    o_ref[...] = acc.astype(o_ref.dtype)


def merge_branches(n, z, w_in, w_bo, tn=512):
    m, d = n.shape
    gate_specs = [pl.BlockSpec((d, tn), functools.partial(
        lambda i, j, b: (0, (MIX_COLS + b * D_MODEL) // tn + j), b=b)) for b in range(N_BRANCH)]
    return pl.pallas_call(
        _merge_kernel,
        grid=(m // TM, d // tn),
        in_specs=[pl.BlockSpec((TM, d), lambda i, j: (i, 0)),
                  pl.BlockSpec((TM, d), lambda i, j: (i, 0))] + gate_specs +
                 [pl.BlockSpec((N_BRANCH, BRANCH_W, tn), lambda i, j: (0, 0, j))],
        out_specs=pl.BlockSpec((TM, tn), lambda i, j: (i, j)),
        out_shape=jax.ShapeDtypeStruct((m, d), BF16),
        compiler_params=_cparams(("parallel", "arbitrary")),
        name="merge_branches",
    )(n, z, w_in, w_in, w_in, w_in, w_bo)


def _resid_mm_kernel(h_ref, x_ref, w_ref, o_ref, *, t_pad, tm):
    y = h_ref[...] + jnp.dot(x_ref[...], w_ref[...], preferred_element_type=F32)
    row = pl.program_id(0) * tm + lax.broadcasted_iota(jnp.int32, (tm, 1), 0)
    o_ref[...] = jnp.where(row % t_pad >= PAD, y, 0.0)


def resid_matmul(h, x, w, t_pad, tm=TM, tn=512):
    m, d = h.shape
    k = x.shape[1]
    return pl.pallas_call(
        functools.partial(_resid_mm_kernel, t_pad=t_pad, tm=tm),
        grid=(m // tm, d // tn),
        in_specs=[pl.BlockSpec((tm, tn), lambda i, j: (i, j)),
                  pl.BlockSpec((tm, k), lambda i, j: (i, 0)),
                  pl.BlockSpec((k, tn), lambda i, j: (0, j))],
        out_specs=pl.BlockSpec((tm, tn), lambda i, j: (i, j)),
        out_shape=jax.ShapeDtypeStruct((m, d), F32),
        compiler_params=_cparams(("parallel", "arbitrary")),
        name="resid_matmul",
    )(h, x, w)


def _ffn_up_kernel(h_ref, g_ref, wg_ref, wu_ref, o_ref, n_ref):
    @pl.when(pl.program_id(1) == 0)
    def _():
        n_ref[...] = _rms_rows(h_ref[...], g_ref[...]).astype(BF16)

    n = n_ref[...]
    a = jnp.dot(n, wg_ref[...], preferred_element_type=F32)
    u = jnp.dot(n, wu_ref[...], preferred_element_type=F32)
    o_ref[...] = (a * jax.nn.sigmoid(a) * u).astype(o_ref.dtype)


def ffn_up(h, gain, w_gate, w_up, tn=512):
    m, d = h.shape
    f = w_gate.shape[1]
    return pl.pallas_call(
        _ffn_up_kernel,
        grid=(m // TM, f // tn),
        in_specs=[pl.BlockSpec((TM, d), lambda i, j: (i, 0)),
                  pl.BlockSpec((1, d), lambda i, j: (0, 0)),
                  pl.BlockSpec((d, tn), lambda i, j: (0, j)),
                  pl.BlockSpec((d, tn), lambda i, j: (0, j))],
        out_specs=pl.BlockSpec((TM, tn), lambda i, j: (i, j)),
        out_shape=jax.ShapeDtypeStruct((m, f), BF16),
        scratch_shapes=[pltpu.VMEM((TM, d), BF16)],
        compiler_params=_cparams(("parallel", "arbitrary")),
        name="ffn_up",
    )(h, gain.reshape(1, d), w_gate, w_up)


def _final_norm_kernel(h_ref, g_ref, o_ref):
    o_ref[...] = _rms_rows(h_ref[...], g_ref[...])


def final_norm(h, gain, bsz, t_pad):
    d = h.shape[-1]
    n_chunks = t_pad // CHUNK
    return pl.pallas_call(
        _final_norm_kernel,
        grid=(bsz, n_chunks - 1),
        in_specs=[pl.BlockSpec((None, CHUNK, d), lambda b, c: (b, c + 1, 0)),
                  pl.BlockSpec((1, d), lambda b, c: (0, 0))],
        out_specs=pl.BlockSpec((None, CHUNK, d), lambda b, c: (b, c, 0)),
        out_shape=jax.ShapeDtypeStruct((bsz, t_pad - CHUNK, d), F32),
        compiler_params=_cparams(("parallel", "parallel")),
        name="final_norm",
    )(h.reshape(bsz, t_pad, d), gain.reshape(1, d))


def _depthwise_conv(x, w, b, pad):
    y = lax.conv_general_dilated(x, w[:, None, :].astype(x.dtype), window_strides=(1,), padding=[pad],
                                 dimension_numbers=('NWC', 'WIO', 'NWC'), feature_group_count=x.shape[-1])
    return y + b.astype(x.dtype)


def _rglru_direction(xc, wa, ba, wx, bx, lam, reverse):
    bsz, t_len, width = xc.shape
    xb = xc.reshape(bsz, t_len, LRU_BLOCKS, LRU_BLOCK_DIM)
    gate_r = jax.nn.sigmoid(jnp.einsum('btki,kij->btkj', xb, wa).reshape(bsz, t_len, width) + ba)
    gate_i = jax.nn.sigmoid(jnp.einsum('btki,kij->btkj', xb, wx).reshape(bsz, t_len, width) + bx)
    log_a = -LRU_C * gate_r * jax.nn.softplus(-lam)
    a = jnp.exp(log_a)
    b = jnp.sqrt(-jnp.expm1(2.0 * log_a)) * gate_i * xc

    def combine(left, right):
        a1, b1 = left
        a2, b2 = right
        return a1 * a2, a2 * b1 + b2

    _, h = lax.associative_scan(combine, (a, b), axis=1, reverse=reverse)
    return h


def _rglru_branch(xa, ga, conv_w, conv_b, wa, ba, wx, bx, lam):
    xc = _depthwise_conv(xa, conv_w, conv_b, LRU_PAD)
    h = (_rglru_direction(xc, wa[0], ba[0], wx[0], bx[0], lam[0], False)
         + _rglru_direction(xc, wa[1], ba[1], wx[1], bx[1], lam[1], True))
    return h * jax.nn.gelu(ga)


def _hyena_filters(t_len, w1, b1, w2, b2, w3, decay):
    n = jnp.arange(t_len, dtype=F32)
    t = n / max(t_len - 1, 1)
    freqs = jnp.linspace(1e-4, HY_BANDS - 1, HY_BANDS, dtype=F32)
    ang = 2.0 * math.pi * n[:, None] * freqs[None, :] / t_len
    z = jnp.concatenate([t[:, None], jnp.cos(ang), -jnp.sin(ang)], axis=-1)
    h = jnp.sin(HY_SIN_FREQ * (z @ w1 + b1))
    h = jnp.sin(HY_SIN_FREQ * (h @ w2 + b2))
    h = h @ w3
    return h * jnp.exp(-t[:, None] * jnp.abs(decay)[None, :])


def _hyena_branch(u3, conv_w, conv_b, w1, b1, w2, b2, w3, decay, bias):
    u3 = _depthwise_conv(u3, conv_w, conv_b, HY_PAD)
    x0, x1, v = jnp.split(u3, 3, axis=-1)
    t_len = u3.shape[1]
    filt = _hyena_filters(t_len, w1, b1, w2, b2, w3, decay)
    h_fwd, h_bwd = filt[:, :BRANCH_W], filt[:, BRANCH_W:]
    k2 = jnp.concatenate([h_fwd, jnp.zeros((1, BRANCH_W), F32), h_bwd[:0:-1]], axis=0)
    u = x1 * v
    n_fft = 2 * t_len
    y = jnp.fft.irfft(jnp.fft.rfft(u, n=n_fft, axis=1) * jnp.fft.rfft(k2, n=n_fft, axis=0)[None],
                      n=n_fft, axis=1)[:, :t_len]
    y = y + u * bias
    return x0 * y


def _to_heads(x, n_heads):
    bsz, t_len, _ = x.shape
    return x.reshape(bsz, t_len, n_heads, -1).transpose(0, 2, 1, 3)


def _from_heads(x):
    bsz, n_heads, t_len, d = x.shape
    return x.transpose(0, 2, 1, 3).reshape(bsz, t_len, n_heads * d)


def _rotary(x):
    t_len, d = x.shape[2], x.shape[3]
    inv = ROPE_BASE ** (-jnp.arange(0, d, 2, dtype=F32) / d)
    ang = jnp.arange(t_len, dtype=F32)[:, None] * inv[None, :]
    cos, sin = jnp.cos(ang), jnp.sin(ang)
    xa, xb = x[..., : d // 2], x[..., d // 2:]
    return jnp.concatenate([xa * cos - xb * sin, xb * cos + xa * sin], axis=-1)


def _gla_chunk_scan(q, k, v, log_f, inclusive):
    bsz, n_heads, t_len, dk = q.shape
    dv = v.shape[-1]
    n_chunks = t_len // CHUNK

    def chunks(a):
        return a.reshape(bsz, n_heads, n_chunks, CHUNK, a.shape[-1]).transpose(2, 0, 1, 3, 4)

    qc, kc, vc = chunks(q), chunks(k), chunks(v)
    bc = jnp.cumsum(chunks(log_f), axis=-2)
    idx = jnp.arange(CHUNK)
    mask = (idx[:, None] >= idx[None, :]) if inclusive else (idx[:, None] > idx[None, :])

    def step(state, inp):
        q_, k_, v_, b_ = inp
        diff = b_[..., :, None, :] - b_[..., None, :, :]
        dec = jnp.exp(jnp.where(mask[:, :, None], diff, -jnp.inf))
        scores = jnp.einsum('bhid,bhjd,bhijd->bhij', q_, k_, dec)
        out = (jnp.einsum('bhij,bhjv->bhiv', scores, v_)
               + jnp.einsum('bhid,bhdv->bhiv', q_ * jnp.exp(b_), state))
        b_last = b_[..., -1:, :]
        state = (jnp.exp(b_last[..., 0, :])[..., None] * state
                 + jnp.einsum('bhjd,bhjv->bhdv', k_ * jnp.exp(b_last - b_), v_))
        return state, out

    s0 = jnp.zeros((bsz, n_heads, dk, dv), F32)
    _, o = lax.scan(step, s0, (qc, kc, vc, bc))
    return o.transpose(1, 2, 0, 3, 4).reshape(bsz, n_heads, t_len, dv)


def _bidirectional_gla(q, k_fwd, k_bwd, v, logf_fwd, logf_bwd, inclusive_bwd):
    def padf(a):
        return jnp.pad(a, ((0, 0), (0, 0), (PAD, 0), (0, 0)))

    def flipf(a):
        return jnp.flip(padf(a), axis=2)

    fwd = _gla_chunk_scan(padf(q), padf(k_fwd), padf(v), padf(logf_fwd), True)
    bwd = jnp.flip(_gla_chunk_scan(flipf(q), flipf(k_bwd), flipf(v), flipf(logf_bwd), inclusive_bwd), axis=2)
    return (fwd + bwd)[:, :, PAD:]


def _retention_branch(q, k, v, g):
    qh = _rotary(_to_heads(q, RET_HEADS))
    kh = _rotary(_to_heads(k, RET_HEADS)) * (RET_DK ** -0.5)
    vh = _to_heads(v, RET_HEADS)
    log_gamma = jnp.log1p(-(2.0 ** (-5.0 - jnp.arange(RET_HEADS, dtype=F32))))
    logf = jnp.broadcast_to(log_gamma[None, :, None, None], qh.shape)
    o = _bidirectional_gla(qh, kh, kh, vh, logf, logf, inclusive_bwd=False)
    mu = jnp.mean(o, axis=-1, keepdims=True)
    o = (o - mu) * lax.rsqrt(jnp.mean((o - mu) ** 2, axis=-1, keepdims=True) + EPS)
    return _from_heads(o) * jax.nn.silu(g)


def _hgrn_lower_bound(lb_logits, layer):
    c = jnp.cumsum(jax.nn.softmax(lb_logits, axis=0), axis=0)
    return c[layer] - c[0]


def _hgrn2_branch(q, f_fwd, f_bwd, i, g, lb):
    qh = _to_heads(jax.nn.silu(q), HG_HEADS)
    ff = lb + (1.0 - lb) * jax.nn.sigmoid(f_fwd)
    fb = lb + (1.0 - lb) * jax.nn.sigmoid(f_bwd)
    o = _bidirectional_gla(qh, _to_heads(1.0 - ff, HG_HEADS), _to_heads(1.0 - fb, HG_HEADS),
                           _to_heads(i, HG_HEADS),
                           _to_heads(jnp.log(ff), HG_HEADS), _to_heads(jnp.log(fb), HG_HEADS), inclusive_bwd=True)
    o = o * lax.rsqrt(jnp.mean(o * o, axis=-1, keepdims=True) + EPS)
    return _from_heads(o) * jax.nn.silu(g)


def _split_mix(p):
    w = BRANCH_W
    sizes = (w, w, 3 * w, w, w, w, w, w, w, w, w, w)
    out, start = [], 0
    for s in sizes:
        out.append(p[..., start:start + s])
        start += s
    return out


def kernel(x, meta, norm_mix, norm_ffn, w_in, lru_conv_w, lru_conv_b, lru_wa, lru_ba, lru_wx, lru_bx, lru_lambda, hy_conv_w, hy_conv_b, hy_w1, hy_b1, hy_w2, hy_b2, hy_w3, hy_decay, hy_bias, hgrn_lb_logits, w_branch_out, w_out, ffn_w_gate, ffn_w_up, ffn_w_down, norm_final):
    bsz, seq, d = x.shape
    depth = w_in.shape[0]
    t_real = N_META + seq
    t_pad = PAD + t_real
    m = bsz * t_pad
    h = jnp.concatenate([jnp.zeros((bsz, PAD, d), F32),
                         jnp.broadcast_to(meta[None].astype(F32), (bsz, N_META, d)), x], axis=1).reshape(m, d)
    for l in range(depth):
        w_in_l = w_in[l].astype(BF16)
        p, n = norm_proj(h, norm_mix[l], w_in_l, MIX_COLS, 512, F32)
        p = p.reshape(bsz, t_pad, MIX_COLS)[:, PAD:]
        (a_x, a_g, b_u, c_q, c_k, c_v, c_g, d_q, d_ff, d_fb, d_i, d_g) = _split_mix(p)
        lb = _hgrn_lower_bound(hgrn_lb_logits, l)
        za = _rglru_branch(a_x, a_g, lru_conv_w[l], lru_conv_b[l], lru_wa[l], lru_ba[l], lru_wx[l],
                           lru_bx[l], lru_lambda[l])
        zb = _hyena_branch(b_u, hy_conv_w[l], hy_conv_b[l], hy_w1[l], hy_b1[l], hy_w2[l], hy_b2[l],
                           hy_w3[l], hy_decay[l], hy_bias[l])
        zc = _retention_branch(c_q, c_k, c_v, c_g)
        zd = _hgrn2_branch(d_q, d_ff, d_fb, d_i, d_g, lb)
        z = jnp.concatenate([za, zb, zc, zd], axis=-1).astype(BF16)
        z = jnp.pad(z, ((0, 0), (PAD, 0), (0, 0))).reshape(m, d)
        merged = merge_branches(n, z, w_in_l, w_branch_out[l].astype(BF16))
        h = resid_matmul(h, merged, w_out[l].astype(BF16), t_pad)
        g = ffn_up(h, norm_ffn[l], ffn_w_gate[l].astype(BF16), ffn_w_up[l].astype(BF16))
        h = resid_matmul(h, g, ffn_w_down[l].astype(BF16), t_pad, tm=TM // 2)
    return final_norm(h, norm_final, bsz, t_pad)
```
